```python
import math
import jax
import jax.numpy as jnp
from jax import lax
import numpy as np

D_MODEL = 4096
BATCH = 4
SEQ = 2048
DEPTH = 4
DEC_BATCH = 8
DEC_SEQ = 4
PAST_LEN = 8192
PAGE_SIZE = 128

A_HEADS = 12
A_DK = 128
A_DV = 128
A_CONV = 4
GDN_CHUNK = 64
A_QK_W = A_HEADS * A_DK
A_V_W = A_HEADS * A_DV
A_QKV_W = 2 * A_QK_W + A_V_W

B_GROUPS = ((128, 1), (512, 4), (2048, 16))
B_HEADS = 8
B_HD = 128
B_W = B_HEADS * B_HD
B_QKV_W = len(B_GROUPS) * 3 * B_W
ROPE_THETA = 10000.0

C_HEADS = 4
C_DK = 192
C_DV = 384
C_RANK = 16
C_TAU = 16.0
GLA_CHUNK = 64
C_QK_W = C_HEADS * C_DK
C_V_W = C_HEADS * C_DV

N_BRANCH = 3
D_FF = 11008
F_CONV = 3
EPS = 1e-6

IN_WIDTHS = (A_QKV_W, A_V_W, A_HEADS, A_HEADS, B_QKV_W,
             C_QK_W, C_QK_W, C_V_W, C_V_W, C_RANK, N_BRANCH * D_MODEL)
N_IN = sum(IN_WIDTHS)

kernel_name = 'hybrid_gdn_dilated_gla_convffn_step'

F32 = jnp.float32


def rms_norm(x, g):
    xf = x.astype(F32)
    y = xf * lax.rsqrt(jnp.mean(xf * xf, axis=-1, keepdims=True) + EPS)
    return (y * g.astype(F32)).astype(x.dtype)


def l2_norm(x):
    return x * lax.rsqrt(jnp.sum(x * x, axis=-1, keepdims=True) + EPS)


def rope(x, pos):
    half = x.shape[-1] // 2
    inv = ROPE_THETA ** (-jnp.arange(half, dtype=F32) / half)
    ang = pos.astype(F32)[:, None] * inv[None, :]
    cos = jnp.cos(ang)[None, :, None, :]
    sin = jnp.sin(ang)[None, :, None, :]
    xf = x.astype(F32)
    x1, x2 = xf[..., :half], xf[..., half:]
    return jnp.concatenate([x1 * cos - x2 * sin, x1 * sin + x2 * cos], axis=-1)


def causal_dwconv(buf, x, w):
    K = w.shape[0]
    T = x.shape[1]
    ext = jnp.concatenate([buf.astype(x.dtype), x], axis=1)
    y = sum(ext[:, j:j + T] * w[j] for j in range(K))
    return y, ext[:, T:]


def _split_columns(u):
    points, acc = [], 0
    for w in IN_WIDTHS[:-1]:
        acc += w
        points.append(acc)
    return jnp.split(u, points, axis=-1)


def _to_chunks(x, C):
    N, T = x.shape[:2]
    nc = -(-T // C)
    x = jnp.pad(x, [(0, 0), (0, nc * C - T)] + [(0, 0)] * (x.ndim - 2))
    x = x.reshape((N, nc, C) + x.shape[2:])
    return jnp.moveaxis(jnp.moveaxis(x, 1, 0), 2, 3)


def _from_chunks(y, T):
    nc, N, H, C = y.shape[:4]
    y = jnp.moveaxis(jnp.moveaxis(y, 3, 2), 0, 1)
    return y.reshape((N, nc * C, H) + y.shape[4:])[:, :T]


def gated_delta_chunked(q, k, v, g, beta, S0):
    T = q.shape[1]
    dv = v.shape[-1]
    C = min(GDN_CHUNK, T)
    xs = tuple(_to_chunks(t, C) for t in (q, k, v, g, beta))
    incl = jnp.tril(jnp.ones((C, C), bool))
    strict = jnp.tril(jnp.ones((C, C), bool), -1)
    eye = jnp.eye(C, dtype=F32)

    def step(S, xc):
        qi, ki, vi, gi, bi = xc
        G = jnp.cumsum(gi, axis=-1)
        decay = jnp.exp(jnp.where(incl, G[..., :, None] - G[..., None, :], -jnp.inf))
        kb = ki * bi[..., None]
        L = jnp.where(strict, jnp.einsum('nhid,nhjd->nhij', kb, ki) * decay, 0.0)
        rhs = jnp.concatenate([vi * bi[..., None], kb * jnp.exp(G)[..., None]], axis=-1)
        sol = lax.linalg.triangular_solve(eye + L, rhs, left_side=True, lower=True, unit_diagonal=True)
        u, w = sol[..., :dv], sol[..., dv:]
        v_new = u - jnp.einsum('nhcd,nhde->nhce', w, S)
        attn = jnp.einsum('nhid,nhjd->nhij', qi, ki) * decay
        o = (jnp.einsum('nhcd,nhde->nhce', qi * jnp.exp(G)[..., None], S)
             + jnp.einsum('nhij,nhje->nhie', attn, v_new))
        G_last = G[..., -1:]
        S = (S * jnp.exp(G_last)[..., None]
             + jnp.einsum('nhcd,nhce->nhde', ki * jnp.exp(G_last - G)[..., None], v_new))
        return S, o

    S, oc = lax.scan(step, S0, xs)
    return _from_chunks(oc, T), S


def gla_chunked(q, k, v, gk, S0):
    T = q.shape[1]
    C = min(GLA_CHUNK, T)
    xs = tuple(_to_chunks(t, C) for t in (q, k, v, gk))
    incl = jnp.tril(jnp.ones((C, C), bool))

    def step(S, xc):
        qi, ki, vi, gi = xc
        B = jnp.cumsum(gi, axis=-2)
        pair = jnp.exp(jnp.where(incl[:, :, None], B[..., :, None, :] - B[..., None, :, :], -jnp.inf))
        attn = jnp.einsum('nhid,nhjd,nhijd->nhij', qi, ki, pair)
        o = (jnp.einsum('nhcd,nhde->nhce', qi * jnp.exp(B), S)
             + jnp.einsum('nhij,nhje->nhie', attn, vi))
        B_last = B[..., -1:, :]
        S = (S * jnp.exp(B[..., -1, :])[..., None]
             + jnp.einsum('nhcd,nhce->nhde', ki * jnp.exp(B_last - B), vi))
        return S, o

    S, oc = lax.scan(step, S0, xs)
    return _from_chunks(oc, T), S


def gdn_branch(a_qkv, a_z, a_b, a_a, conv_w, A_log, dt_bias, norm_g, conv_buf, S0):
    N, T, _ = a_qkv.shape
    hc, new_buf = causal_dwconv(conv_buf, a_qkv, conv_w)
    hc = jax.nn.silu(hc.astype(F32))
    q = l2_norm(hc[..., :A_QK_W].reshape(N, T, A_HEADS, A_DK)) * (A_DK ** -0.5)
    k = l2_norm(hc[..., A_QK_W:2 * A_QK_W].reshape(N, T, A_HEADS, A_DK))
    v = hc[..., 2 * A_QK_W:].reshape(N, T, A_HEADS, A_DV)
    beta = jax.nn.sigmoid(a_b.astype(F32))
    g = -jnp.exp(A_log.astype(F32)) * jax.nn.softplus(a_a.astype(F32) + dt_bias.astype(F32))
    o, S = gated_delta_chunked(q, k, v, g, beta, S0.astype(F32))
    o = rms_norm(o, norm_g) * jax.nn.silu(a_z.astype(F32).reshape(N, T, A_HEADS, A_DV))
    return o.reshape(N, T, A_V_W), new_buf, S


def gla_branch(c_q, c_k, c_v, c_r, c_glr, w_gate2, b_gate, norm_g, S0):
    N, T, _ = c_q.shape
    q = c_q.astype(F32).reshape(N, T, C_HEADS, C_DK) * (C_DK ** -0.5)
    k = c_k.astype(F32).reshape(N, T, C_HEADS, C_DK)
    v = c_v.astype(F32).reshape(N, T, C_HEADS, C_DV)
    gk = jax.nn.log_sigmoid(c_glr.astype(F32) @ w_gate2.astype(F32) + b_gate.astype(F32)) / C_TAU
    o, S = gla_chunked(q, k, v, gk.reshape(N, T, C_HEADS, C_DK), S0.astype(F32))
    o = rms_norm(o, norm_g) * jax.nn.silu(c_r.astype(F32).reshape(N, T, C_HEADS, C_DV))
    return o.reshape(N, T, C_V_W), S


def _softmax_lse(s):
    m = jnp.max(s, axis=-1, keepdims=True)
    p = jnp.exp(s - m)
    l = jnp.sum(p, axis=-1, keepdims=True)
    return p / l, (m + jnp.log(l))[..., 0]


def _local_window_attn(q, k, v, n_back):
    M, n, H, hd = q.shape
    blk = n_back
    nb = -(-n // blk)
    pad = nb * blk - n

    def blocks(t):
        return jnp.pad(t, ((0, 0), (0, pad), (0, 0), (0, 0))).reshape(M, nb, blk, H, hd)

    def with_prev(t):
        prev = jnp.concatenate([jnp.zeros_like(t[:, :1]), t[:, :-1]], axis=1)
        return jnp.concatenate([prev, t], axis=2)

    qb = blocks(q)
    kk = with_prev(blocks(k))
    vv = with_prev(blocks(v))
    s = jnp.einsum('mbqhd,mbkhd->mbhqk', qb, kk) * (hd ** -0.5)
    qi = jnp.arange(blk)[:, None]
    kj = jnp.arange(2 * blk)[None, :]
    dist = blk + qi - kj
    kpos = jnp.arange(nb)[:, None, None] * blk - blk + kj[None]
    valid = (dist >= 0) & (dist <= n_back) & (kpos >= 0)
    s = jnp.where(valid[None, :, None], s, -jnp.inf)
    p, lse = _softmax_lse(s)
    o = jnp.einsum('mbhqk,mbkhd->mbqhd', p, vv).reshape(M, nb * blk, H, hd)[:, :n]
    lse = jnp.moveaxis(lse, 2, 3).reshape(M, nb * blk, H)[:, :n]
    return o, lse


def _dilated_attn_prompt(q, k, v, dil, n_back):
    N, S, H, hd = q.shape
    n = S // dil

    def fold(t):
        return t.reshape(N, n, dil, H, hd).transpose(0, 2, 1, 3, 4).reshape(N * dil, n, H, hd)

    o, lse = _local_window_attn(fold(q), fold(k), fold(v), n_back)
    o = o.reshape(N, dil, n, H, hd).transpose(0, 2, 1, 3, 4).reshape(N, S, H, hd)
    lse = lse.reshape(N, dil, n, H).transpose(0, 2, 1, 3).reshape(N, S, H)
    return o, lse


def _dilated_attn_sample(q, k, v, kv_buf, dil, n_back):
    T = q.shape[1]
    L = kv_buf.shape[1]
    k_ext = jnp.concatenate([kv_buf[:, :, 0].astype(F32), k], axis=1)
    v_ext = jnp.concatenate([kv_buf[:, :, 1].astype(F32), v], axis=1)
    idx = L + jnp.arange(T)[:, None] - dil * jnp.arange(n_back + 1)[None, :]
    valid = idx >= 0
    idx = jnp.maximum(idx, 0)
    kg = k_ext[:, idx]
    vg = v_ext[:, idx]
    s = jnp.einsum('nthd,ntjhd->nhtj', q, kg) * (q.shape[-1] ** -0.5)
    s = jnp.where(valid[None, None], s, -jnp.inf)
    p, lse = _softmax_lse(s)
    o = jnp.einsum('nhtj,ntjhd->nthd', p, vg)
    return o, jnp.transpose(lse, (0, 2, 1))


def dilated_branch(b_qkv, pos, kv_bufs):
    N, T, _ = b_qkv.shape
    qkv = b_qkv.reshape(N, T, len(B_GROUPS), 3, B_HEADS, B_HD)
    outs, lses, new_rows = [], [], []
    for gi, (window, dil) in enumerate(B_GROUPS):
        n_back = window // dil
        q = rope(qkv[:, :, gi, 0], pos)
        k = rope(qkv[:, :, gi, 1], pos)
        v = qkv[:, :, gi, 2].astype(F32)
        if kv_bufs is None:
            o, lse = _dilated_attn_prompt(q, k, v, dil, n_back)
            keep = min(window, T)
            rows = jnp.stack([k[:, T - keep:], v[:, T - keep:]], axis=2)
        else:
            o, lse = _dilated_attn_sample(q, k, v, kv_bufs[gi], dil, n_back)
            rows = jnp.stack([k, v], axis=2)
        outs.append(o)
        lses.append(lse)
        new_rows.append(rows.astype(b_qkv.dtype))
    wts = jax.nn.softmax(jnp.stack(lses), axis=0)
    o = jnp.einsum('gnth,gnthd->nthd', wts, jnp.stack(outs))
    return o.reshape(N, T, B_W), new_rows


def _trunk(x, c, pos, states, weights):
    (norm1_g, norm2_g, w_ada, b_ada, w_in, a_conv_w, a_A_log, a_dt_bias, a_norm_g,
     c_w_gate2, c_b_gate, c_norm_g, w_br_a, w_br_b, w_br_c, w_o,
     f_up, f_conv_w, f_conv_b, f_down, final_g) = weights
    N, T, _ = x.shape
    dt = x.dtype
    new_aS, new_aconv, new_cS, new_fconv = [], [], [], []
    new_b = [[] for _ in B_GROUPS]
    for l in range(DEPTH):
        if states is None:
            aS0 = jnp.zeros((N, A_HEADS, A_DK, A_DV), F32)
            abuf0 = jnp.zeros((N, A_CONV - 1, A_QKV_W), dt)
            bbufs = None
            cS0 = jnp.zeros((N, C_HEADS, C_DK, C_DV), F32)
            fbuf0 = jnp.zeros((N, F_CONV - 1, 2 * D_FF), dt)
        else:
            st_aS, st_aconv, st_b, st_cS, st_fconv = states
            aS0, abuf0, cS0, fbuf0 = st_aS[l], st_aconv[l], st_cS[l], st_fconv[l]
            bbufs = [b[l] for b in st_b]

        mod = (jax.nn.silu(c) @ w_ada[l] + b_ada[l]).reshape(N, 6, 1, D_MODEL).astype(dt)
        shift1, scale1, gate1, shift2, scale2, gate2 = (mod[:, i] for i in range(6))

        h = rms_norm(x, norm1_g[l]) * (1 + scale1) + shift1
        (a_qkv, a_z, a_b, a_a, b_qkv, c_q, c_k, c_v, c_r, c_glr, br_gates) = _split_columns(h @ w_in[l])
        oa, abuf, aS = gdn_branch(a_qkv, a_z, a_b, a_a, a_conv_w[l], a_A_log[l], a_dt_bias[l],
                                  a_norm_g[l], abuf0, aS0)
        ob, brows = dilated_branch(b_qkv, pos, bbufs)
        oc, cS = gla_branch(c_q, c_k, c_v, c_r, c_glr, c_w_gate2[l], c_b_gate[l], c_norm_g[l], cS0)
        gts = jax.nn.sigmoid(br_gates.astype(F32)).reshape(N, T, N_BRANCH, D_MODEL)
        merged = (gts[:, :, 0] * (oa @ w_br_a[l]) + gts[:, :, 1] * (ob @ w_br_b[l])
                  + gts[:, :, 2] * (oc @ w_br_c[l]))
        x = x + (gate1 * (merged.astype(dt) @ w_o[l])).astype(dt)

        h2 = rms_norm(x, norm2_g[l]) * (1 + scale2) + shift2
        up, fbuf = causal_dwconv(fbuf0, h2 @ f_up[l], f_conv_w[l])
        up = up + f_conv_b[l]
        ffn = (jax.nn.silu(up[..., :D_FF]) * up[..., D_FF:]) @ f_down[l]
        x = x + (gate2 * ffn).astype(dt)

        new_aS.append(aS.astype(dt))
        new_aconv.append(abuf)
        for gi in range(len(B_GROUPS)):
            new_b[gi].append(brows[gi])
        new_cS.append(cS.astype(dt))
        new_fconv.append(fbuf)
    y = rms_norm(x, final_g)
    return y, (jnp.stack(new_aS), jnp.stack(new_aconv), jnp.stack(new_b[0]), jnp.stack(new_b[1]),
               jnp.stack(new_b[2]), jnp.stack(new_cS), jnp.stack(new_fconv))


def setup_inputs(seed: int = 0) -> dict:
    key = jax.random.key(seed)
    ks = iter(jax.random.split(key, 48))

    def nrm(shape, scale):
        return jax.random.normal(next(ks), shape, jnp.float32) * scale

    def gain(shape):
        return 1.0 + nrm(shape, 0.02)

    kv_len = [min(w, PAST_LEN) for (w, _) in B_GROUPS]
    dt_init = jnp.exp(jax.random.uniform(next(ks), (DEPTH, A_HEADS), jnp.float32,
                                         math.log(1e-3), math.log(1e-1)))
    return {
        'x_prompt': nrm((BATCH, SEQ, D_MODEL), 1.0),
        'x_sample': nrm((DEC_BATCH, DEC_SEQ, D_MODEL), 1.0),
        'state_a_S': nrm((DEPTH, DEC_BATCH, A_HEADS, A_DK, A_DV), 0.1),
        'state_a_conv': nrm((DEPTH, DEC_BATCH, A_CONV - 1, A_QKV_W), 1.0),
        'cache_b1_kv': nrm((DEPTH, DEC_BATCH, kv_len[0], 2, B_HEADS, B_HD), 1.0),
        'cache_b2_kv': nrm((DEPTH, DEC_BATCH, kv_len[1], 2, B_HEADS, B_HD), 1.0),
        'cache_b3_kv': nrm((DEPTH, DEC_BATCH, kv_len[2], 2, B_HEADS, B_HD), 1.0),
        'state_c_S': nrm((DEPTH, DEC_BATCH, C_HEADS, C_DK, C_DV), 1.0),
        'state_ffn_conv': nrm((DEPTH, DEC_BATCH, F_CONV - 1, 2 * D_FF), 1.0),
        'c_prompt': nrm((BATCH, D_MODEL), 1.0),
        'c_sample': nrm((DEC_BATCH, D_MODEL), 1.0),
        'norm1_g': gain((DEPTH, D_MODEL)),
        'norm2_g': gain((DEPTH, D_MODEL)),
        'w_ada': nrm((DEPTH, D_MODEL, 6 * D_MODEL), 0.5 * D_MODEL ** -0.5),
        'b_ada': nrm((DEPTH, 6 * D_MODEL), 0.02),
        'w_in': nrm((DEPTH, D_MODEL, N_IN), D_MODEL ** -0.5),
        'a_conv_w': nrm((DEPTH, A_CONV, A_QKV_W), A_CONV ** -0.5),
        'a_A_log': jnp.log(jax.random.uniform(next(ks), (DEPTH, A_HEADS), jnp.float32, 1.0, 16.0)),
        'a_dt_bias': dt_init + jnp.log(-jnp.expm1(-dt_init)),
        'a_norm_g': gain((DEPTH, A_DV)),
        'c_w_gate2': nrm((DEPTH, C_RANK, C_QK_W), C_RANK ** -0.5),
        'c_b_gate': nrm((DEPTH, C_QK_W), 0.02),
        'c_norm_g': gain((DEPTH, C_DV)),
        'w_br_a': nrm((DEPTH, A_V_W, D_MODEL), A_V_W ** -0.5),
        'w_br_b': nrm((DEPTH, B_W, D_MODEL), B_W ** -0.5),
        'w_br_c': nrm((DEPTH, C_V_W, D_MODEL), C_V_W ** -0.5),
        'w_o': nrm((DEPTH, D_MODEL, D_MODEL), D_MODEL ** -0.5),
        'f_up': nrm((DEPTH, D_MODEL, 2 * D_FF), D_MODEL ** -0.5),
        'f_conv_w': nrm((DEPTH, F_CONV, 2 * D_FF), F_CONV ** -0.5),
        'f_conv_b': nrm((DEPTH, 2 * D_FF), 0.02),
        'f_down': nrm((DEPTH, D_FF, D_MODEL), D_FF ** -0.5),
        'final_g': gain((D_MODEL,)),
    }


def reference(x_prompt, x_sample, state_a_S, state_a_conv, cache_b1_kv, cache_b2_kv, cache_b3_kv,
              state_c_S, state_ffn_conv, c_prompt, c_sample,
              norm1_g, norm2_g, w_ada, b_ada, w_in, a_conv_w, a_A_log, a_dt_bias, a_norm_g,
              c_w_gate2, c_b_gate, c_norm_g, w_br_a, w_br_b, w_br_c, w_o,
              f_up, f_conv_w, f_conv_b, f_down, final_g):
    weights = (norm1_g, norm2_g, w_ada, b_ada, w_in, a_conv_w, a_A_log, a_dt_bias, a_norm_g,
               c_w_gate2, c_b_gate, c_norm_g, w_br_a, w_br_b, w_br_c, w_o,
               f_up, f_conv_w, f_conv_b, f_down, final_g)
    pos_prompt = jnp.arange(x_prompt.shape[1], dtype=jnp.int32)
    pos_sample = PAST_LEN + jnp.arange(x_sample.shape[1], dtype=jnp.int32)
    y_prompt, (p_a_S, p_a_conv, p_b1_kv, p_b2_kv, p_b3_kv, p_c_S, p_ffn_conv) = _trunk(
        x_prompt, c_prompt, pos_prompt, None, weights)
    states = (state_a_S, state_a_conv, (cache_b1_kv, cache_b2_kv, cache_b3_kv), state_c_S, state_ffn_conv)
    y_sample, (s_a_S, s_a_conv, s_b1_kv, s_b2_kv, s_b3_kv, s_c_S, s_ffn_conv) = _trunk(
        x_sample, c_sample, pos_sample, states, weights)
    return (y_prompt, y_sample,
            p_a_S, p_a_conv, p_b1_kv, p_b2_kv, p_b3_kv, p_c_S, p_ffn_conv,
            s_a_S, s_a_conv, s_b1_kv, s_b2_kv, s_b3_kv, s_c_S, s_ffn_conv)
```

```python
import functools

import jax
import jax.numpy as jnp
from jax import lax
from jax.experimental import pallas as pl
from jax.experimental.pallas import tpu as pltpu

F32 = jnp.float32
BF16 = jnp.bfloat16
HI = lax.Precision.HIGHEST

D_MODEL = 4096
DEPTH = 4
PAST_LEN = 8192
EPS = 1e-6

A_HEADS, A_DK, A_DV, A_CONV = 12, 128, 128, 4
A_QK_W = A_HEADS * A_DK
A_V_W = A_HEADS * A_DV
A_QKV_W = 2 * A_QK_W + A_V_W

B_GROUPS = ((128, 1), (512, 4), (2048, 16))
B_HEADS, B_HD = 8, 128
B_W = B_HEADS * B_HD
B_QKV_W = len(B_GROUPS) * 3 * B_W
B_NBACK = 128
ROPE_THETA = 10000.0

C_HEADS, C_DK, C_DV, C_RANK, C_TAU = 4, 192, 384, 16, 16.0
C_DKP = 256
C_QK_W = C_HEADS * C_DK
C_V_W = C_HEADS * C_DV

N_BRANCH = 3
D_FF = 11008
F_CONV = 3
CHUNK = 64

LANES = 128
SUBLANES = 8

OFF_AQKV = 0
OFF_CV = OFF_AQKV + A_QKV_W
OFF_CR = OFF_CV + C_V_W
OFF_AZ = OFF_CR + C_V_W
OFF_B = OFF_AZ + A_V_W
OFF_CQ = OFF_B + B_QKV_W
OFF_CK = OFF_CQ + C_HEADS * C_DKP
OFF_GATES = OFF_CK + C_HEADS * C_DKP
OFF_SMALL = OFF_GATES + N_BRANCH * D_MODEL
SM_AB, SM_AA, SM_GLR = 0, A_HEADS, 2 * A_HEADS
N_INP = 33024
TN_IN = 768

VMEM_LIMIT = 56 << 20


def _params(sem, vmem=VMEM_LIMIT):
    return pltpu.CompilerParams(dimension_semantics=sem, vmem_limit_bytes=vmem)


def _dot(a, b, precision=None):
    return jnp.dot(a, b, precision=precision, preferred_element_type=F32)


def _dot_nt(a, b, precision=None):
    return lax.dot_general(a, b, (((1,), (1,)), ((), ())), precision=precision, preferred_element_type=F32)


def _dot_tn(a, b, precision=None):
    return lax.dot_general(a, b, (((0,), (0,)), ((), ())), precision=precision, preferred_element_type=F32)


def _silu(x):
    return x * jax.nn.sigmoid(x)


def _norm_body(x_ref, g_ref, *rest, modulate, per_row):
    if modulate:
        sc_ref, sh_ref, o_ref = rest
    else:
        (o_ref,) = rest
    x = x_ref[...]
    y = x * lax.rsqrt(jnp.mean(x * x, axis=-1, keepdims=True) + EPS) * g_ref[...]
    if modulate:
        sc = sc_ref[...] if per_row else sc_ref[0]
        sh = sh_ref[...] if per_row else sh_ref[0]
        y = y * (1.0 + sc) + sh
    o_ref[...] = y.astype(o_ref.dtype)


def _norm(x, g, scale, shift, *, seq_len, out_dtype):
    M, D = x.shape
    modulate = scale is not None
    tm = min(256, M)
    per_row = modulate and scale.ndim == 2
    in_specs = [pl.BlockSpec((tm, D), lambda i: (i, 0)), pl.BlockSpec((1, D), lambda i: (0, 0))]
    args = [x, g.reshape(1, D)]
    if modulate:
        if per_row:
            spec = pl.BlockSpec((tm, D), lambda i: (i, 0))
        else:
            spec = pl.BlockSpec((1, 1, D), lambda i: ((i * tm) // seq_len, 0, 0))
        in_specs += [spec, spec]
        args += [scale, shift]
    return pl.pallas_call(
        functools.partial(_norm_body, modulate=modulate, per_row=per_row),
        grid=(M // tm,),
        in_specs=in_specs,
        out_specs=pl.BlockSpec((tm, D), lambda i: (i, 0)),
        out_shape=jax.ShapeDtypeStruct((M, D), out_dtype),
        compiler_params=_params(("parallel",)),
        name="norm_mod",
    )(*args)


def _mm_body(a_ref, w_ref, *rest, a_silu, has_bias, has_resid, gate_per_row):
    rest = list(rest)
    bias_ref = rest.pop(0) if has_bias else None
    resid_ref = rest.pop(0) if has_resid else None
    gate_ref = rest.pop(0) if has_resid else None
    o_ref = rest.pop(0)
    a = a_ref[...]
    if a_silu:
        a = _silu(a).astype(BF16)
    acc = _dot(a, w_ref[...])
    if has_bias:
        acc = acc + bias_ref[...]
    if has_resid:
        gate = gate_ref[...] if gate_per_row else gate_ref[0]
        acc = resid_ref[...] + gate * acc
    o_ref[...] = acc.astype(o_ref.dtype)


def _matmul(a, w, layer, *, tm, tn, seq_len=None, a_silu=False, bias=None, resid=None, gate=None,
            out_dtype=F32, name="matmul"):
    M, K = a.shape
    N = w.shape[-1]
    tm = min(tm, M)
    has_bias, has_resid = bias is not None, resid is not None
    gate_per_row = has_resid and gate.ndim == 2
    in_specs = [pl.BlockSpec((tm, K), lambda i, j: (i, 0)),
                pl.BlockSpec((None, K, tn), lambda i, j: (layer, 0, j))]
    args = [a, w]
    if has_bias:
        in_specs.append(pl.BlockSpec((None, 1, tn), lambda i, j: (layer, 0, j)))
        args.append(bias)
    if has_resid:
        in_specs.append(pl.BlockSpec((tm, tn), lambda i, j: (i, j)))
        if gate_per_row:
            in_specs.append(pl.BlockSpec((tm, tn), lambda i, j: (i, j)))
        else:
            in_specs.append(pl.BlockSpec((1, 1, tn), lambda i, j: ((i * tm) // seq_len, 0, j)))
        args += [resid, gate]
    return pl.pallas_call(
        functools.partial(_mm_body, a_silu=a_silu, has_bias=has_bias, has_resid=has_resid,
                          gate_per_row=gate_per_row),
        grid=(M // tm, N // tn),
        in_specs=in_specs,
        out_specs=pl.BlockSpec((tm, tn), lambda i, j: (i, j)),
        out_shape=jax.ShapeDtypeStruct((M, N), out_dtype),
        compiler_params=_params(("parallel", "arbitrary")),
        name=name,
    )(*args)


def _merge_body(oa_ref, ob_ref, oc_ref, wa_ref, wb_ref, wc_ref, ga_ref, gb_ref, gc_ref, o_ref):
    acc = jax.nn.sigmoid(ga_ref[...]) * _dot(oa_ref[...], wa_ref[...])
    acc = acc + jax.nn.sigmoid(gb_ref[...]) * _dot(ob_ref[...], wb_ref[...])
    acc = acc + jax.nn.sigmoid(gc_ref[...]) * _dot(oc_ref[...], wc_ref[...])
    o_ref[...] = acc.astype(o_ref.dtype)


def _merge(oa, ob, oc, wa, wb, wc, u, layer, *, tm, tn):
    M = oa.shape[0]
    tm = min(tm, M)
    gate0 = OFF_GATES // tn
    gstep = D_MODEL // tn

    def a_spec(width):
        return pl.BlockSpec((tm, width), lambda i, j: (i, 0))

    def w_spec(width):
        return pl.BlockSpec((None, width, tn), lambda i, j: (layer, 0, j))

    def g_spec(b):
        return pl.BlockSpec((tm, tn), lambda i, j: (i, gate0 + b * gstep + j))

    return pl.pallas_call(
        _merge_body,
        grid=(M // tm, D_MODEL // tn),
        in_specs=[a_spec(A_V_W), a_spec(B_W), a_spec(C_V_W), w_spec(A_V_W), w_spec(B_W), w_spec(C_V_W),
                  g_spec(0), g_spec(1), g_spec(2)],
        out_specs=pl.BlockSpec((tm, tn), lambda i, j: (i, j)),
        out_shape=jax.ShapeDtypeStruct((M, D_MODEL), BF16),
        compiler_params=_params(("parallel", "arbitrary")),
        name="branch_merge",
    )(oa, ob, oc, wa, wb, wc, u, u, u)


def _ffn_act_body(ug_ref, uv_ref, wg_ref, wv_ref, bg_ref, bv_ref, fg_ref, fv_ref, o_ref, ext_ref, *, tb):
    r = pl.program_id(2)

    @pl.when(r == 0)
    def _():
        ext_ref[0, 6:8, :] = fg_ref[0]
        ext_ref[1, 6:8, :] = fv_ref[0]

    def conv(i, x_ref, w_ref, b_ref):
        ext_ref[i, 8:8 + tb, :] = x_ref[...]
        w = w_ref[...]
        y = (w[2:3] * ext_ref[i, 8:8 + tb, :] + w[1:2] * ext_ref[i, 7:7 + tb, :]
             + w[0:1] * ext_ref[i, 6:6 + tb, :]) + b_ref[...]
        ext_ref[i, 0:8, :] = ext_ref[i, tb:tb + 8, :]
        return y

    g = conv(0, ug_ref, wg_ref, bg_ref)
    v = conv(1, uv_ref, wv_ref, bv_ref)
    o_ref[...] = (_silu(g) * v).astype(o_ref.dtype)


def _ffn_act(up, conv_w, conv_b, fbuf, layer, *, n_seq, seq_len):
    tc = 256
    ncol = D_FF // tc
    tb = min(512, seq_len)
    nrow = seq_len // tb
    M = up.shape[0]

    def x_spec(off):
        return pl.BlockSpec((tb, tc), lambda n, j, r: (n * nrow + r, off + j))

    def w_spec(rows, off):
        return pl.BlockSpec((None, rows, tc), lambda n, j, r: (layer, 0, off + j))

    def f_spec(off):
        return pl.BlockSpec((1, F_CONV - 1, tc), lambda n, j, r: (n, 0, off + j))

    return pl.pallas_call(
        functools.partial(_ffn_act_body, tb=tb),
        grid=(n_seq, ncol, nrow),
        in_specs=[x_spec(0), x_spec(ncol), w_spec(F_CONV, 0), w_spec(F_CONV, ncol), w_spec(1, 0), w_spec(1, ncol),
                  f_spec(0), f_spec(ncol)],
        out_specs=pl.BlockSpec((tb, tc), lambda n, j, r: (n * nrow + r, j)),
        out_shape=jax.ShapeDtypeStruct((M, D_FF), BF16),
        scratch_shapes=[pltpu.VMEM((2, tb + 8, tc), F32)],
        compiler_params=_params(("parallel", "parallel", "arbitrary")),
        name="ffn_conv_act",
    )(up, up, conv_w, conv_w, conv_b, conv_b, fbuf, fbuf)


def _chunk_iotas(C):
    i = lax.broadcasted_iota(jnp.int32, (C, C), 0)
    j = lax.broadcasted_iota(jnp.int32, (C, C), 1)
    return i, j


def _unit_lower_inverse(L, iC, jC, eye):
    C = L.shape[0]
    Mb = jnp.where((iC >> 4) == (jC >> 4), -L, 0.0)
    inv = eye + Mb
    P = Mb
    for _ in range(3):
        P = _dot(P, P, HI)
        inv = inv + _dot(inv, P, HI)
    s = 16
    while s < C:
        sh = s.bit_length() - 1
        off = jnp.where(((iC >> (sh + 1)) == (jC >> (sh + 1))) & ((iC >> sh) != (jC >> sh)), L, 0.0)
        inv = inv - _dot(inv, _dot(off, inv, HI), HI)
        s *= 2
    return inv


def _gdn_body(q_ref, k_ref, v_ref, z_ref, ab_ref, aa_ref, cwq_ref, cwk_ref, cwv_ref, bq_ref, bk_ref, bv_ref,
              alog_ref, dtb_ref, ng_ref, s0_ref, o_ref, sout_ref, ext_ref, S_ref, *, rows_in, TB, t_valid):
    C = CHUNK
    h = pl.program_id(1)
    b = pl.program_id(2)
    nb = pl.num_programs(2)

    @pl.when(b == 0)
    def _():
        S_ref[...] = s0_ref[0, 0]
        if rows_in < TB:
            ext_ref[...] = jnp.zeros(ext_ref.shape, F32)
        ext_ref[0, 5:8, :] = bq_ref[0]
        ext_ref[1, 5:8, :] = bk_ref[0]
        ext_ref[2, 5:8, :] = bv_ref[0]

    def conv_silu(i, x_ref, cw_ref):
        ext_ref[i, 8:8 + rows_in, :] = x_ref[...]
        w = cw_ref[...]
        y = (w[3:4] * ext_ref[i, 8:8 + TB, :] + w[2:3] * ext_ref[i, 7:7 + TB, :]
             + w[1:2] * ext_ref[i, 6:6 + TB, :] + w[0:1] * ext_ref[i, 5:5 + TB, :])
        ext_ref[i, 0:8, :] = ext_ref[i, TB:TB + 8, :]
        return _silu(y)

    def l2n(x):
        return x * lax.rsqrt(jnp.sum(x * x, axis=-1, keepdims=True) + EPS)

    q = l2n(conv_silu(0, q_ref, cwq_ref)) * (A_DK ** -0.5)
    k = l2n(conv_silu(1, k_ref, cwk_ref))
    v = conv_silu(2, v_ref, cwv_ref)

    zero_row = jnp.zeros((1, TB), F32)
    beta_row = jax.nn.sigmoid(ab_ref[0, 0])
    x = aa_ref[0, 0] + (zero_row + dtb_ref[h])
    softplus = jnp.maximum(x, 0.0) + jnp.log(1.0 + jnp.exp(-jnp.abs(x)))
    g_row = -jnp.exp(zero_row + alog_ref[h]) * softplus
    if t_valid is not None:
        live = lax.broadcasted_iota(jnp.int32, (1, TB), 1) < t_valid
        beta_row = jnp.where(live, beta_row, 0.0)
        g_row = jnp.where(live, g_row, 0.0)

    iC, jC = _chunk_iotas(C)
    incl = iC >= jC
    strict = iC > jC
    eye = (iC == jC).astype(F32)
    upper = (iC <= jC).astype(F32)
    ones_cl = jnp.ones((C, LANES), F32)
    ng = ng_ref[...]

    for c in range(TB // C):
        lo = c * C
        qc, kc, vc = q[lo:lo + C], k[lo:lo + C], v[lo:lo + C]
        g_c = g_row[:, lo:lo + C]
        b_c = beta_row[:, lo:lo + C]
        G_row = _dot(jnp.broadcast_to(g_c, (SUBLANES, C)), upper, HI)[0:1]
        G_col = _dot(eye * G_row, ones_cl, HI)
        b_col = _dot(eye * b_c, ones_cl, HI)
        diff = G_col[:, :C] - G_row
        decay = jnp.where(incl, jnp.exp(jnp.minimum(diff, 0.0)), 0.0)
        kb = kc * b_col
        L = jnp.where(strict, _dot_nt(kb, kc, HI) * decay, 0.0)
        eG = jnp.exp(G_col)
        rhs = jnp.concatenate([vc * b_col, kb * eG], axis=1)
        sol = _dot(_unit_lower_inverse(L, iC, jC, eye), rhs, HI)
        S = S_ref[...]
        v_new = sol[:, :A_DV] - _dot(sol[:, A_DV:], S, HI)
        attn = _dot_nt(qc, kc, HI) * decay
        o = _dot(qc * eG, S, HI) + _dot(attn, v_new, HI)
        G_last = G_col[C - 1:C, :]
        S_ref[...] = S * jnp.exp(G_last) + _dot_tn(kc * jnp.exp(G_last - G_col), v_new, HI)
        rows = min(C, rows_in)
        o = o[:rows]
        o = o * lax.rsqrt(jnp.mean(o * o, axis=-1, keepdims=True) + EPS) * ng
        o_ref[lo:lo + rows, :] = (o * _silu(z_ref[lo:lo + rows, :])).astype(o_ref.dtype)

    @pl.when(b == nb - 1)
    def _():
        sout_ref[0, 0] = S_ref[...]


def _gdn(u, ab_t, aa_t, conv_w, conv_buf, a_log, dt_bias, norm_g, s0, layer, *, n_seq, rows_in, TB, t_valid):
    M = u.shape[0]
    nb = (M // n_seq) // rows_in
    H = A_HEADS

    def x_spec(off):
        return pl.BlockSpec((rows_in, LANES), lambda n, h, b: (n * nb + b, off + h))

    def t_spec():
        return pl.BlockSpec((1, 1, 1, TB), lambda n, h, b: (n, h, 0, b))

    def cw_spec(off):
        return pl.BlockSpec((None, A_CONV, LANES), lambda n, h, b: (layer, 0, off + h))

    def buf_spec(off):
        return pl.BlockSpec((1, A_CONV - 1, LANES), lambda n, h, b: (n, 0, off + h))

    smem = pl.BlockSpec(memory_space=pltpu.SMEM)
    return pl.pallas_call(
        functools.partial(_gdn_body, rows_in=rows_in, TB=TB, t_valid=t_valid),
        grid=(n_seq, H, nb),
        in_specs=[x_spec(0), x_spec(H), x_spec(2 * H), x_spec(OFF_AZ // LANES), t_spec(), t_spec(),
                  cw_spec(0), cw_spec(H), cw_spec(2 * H), buf_spec(0), buf_spec(H), buf_spec(2 * H),
                  smem, smem, pl.BlockSpec((1, A_DV), lambda n, h, b: (0, 0)),
                  pl.BlockSpec((1, 1, A_DK, A_DV), lambda n, h, b: (n, h, 0, 0))],
        out_specs=[pl.BlockSpec((rows_in, LANES), lambda n, h, b: (n * nb + b, h)),
                   pl.BlockSpec((1, 1, A_DK, A_DV), lambda n, h, b: (n, h, 0, 0))],
        out_shape=[jax.ShapeDtypeStruct((M, A_V_W), BF16),
                   jax.ShapeDtypeStruct((n_seq, H, A_DK, A_DV), F32)],
        scratch_shapes=[pltpu.VMEM((3, TB + 8, LANES), F32), pltpu.VMEM((A_DK, A_DV), F32)],
        compiler_params=_params(("parallel", "parallel", "arbitrary")),
        name="gdn_mixer",
    )(u, u, u, u, ab_t, aa_t, conv_w, conv_w, conv_w, conv_buf, conv_buf, conv_buf,
      a_log, dt_bias, norm_g.reshape(1, A_DV), s0)


GLA_LEVELS = (32, 16, 8, 4, 2, 1)


def _gla_body(q_ref, k_ref, v_ref, r_ref, sm_ref, wg_ref, bg_ref, ng_ref, s0_ref, o_ref, sout_ref, St_ref,
              *, rows_in, TB, t_valid):
    C = CHUNK
    b = pl.program_id(2)
    nb = pl.num_programs(2)

    @pl.when(b == 0)
    def _():
        St_ref[...] = s0_ref[0, 0]

    q = q_ref[...] * (C_DK ** -0.5)
    k = k_ref[...]
    v = v_ref[...]
    x = _dot(sm_ref[...], wg_ref[0], HI) + bg_ref[0]
    gk = (jnp.minimum(x, 0.0) - jnp.log(1.0 + jnp.exp(-jnp.abs(x)))) * (1.0 / C_TAU)
    if t_valid is not None:
        live = lax.broadcasted_iota(jnp.int32, (rows_in, 1), 0) < t_valid
        gk = jnp.where(live, gk, 0.0)
        k = jnp.where(live, k, 0.0)
        v = jnp.where(live, v, 0.0)
    if rows_in < TB:
        def pad(t):
            return jnp.concatenate([t, jnp.zeros((TB - rows_in, t.shape[1]), F32)], axis=0)
        q, k, v, gk = pad(q), pad(k), pad(v), pad(gk)

    iC, jC = _chunk_iotas(C)
    eye = iC == jC
    lower = (iC >= jC).astype(F32)
    sel, masks = [], []
    for s in GLA_LEVELS:
        sh = s.bit_length()
        boundary = ((iC >> sh) << sh) + (s - 1)
        sel.append((jC == boundary).astype(F32))
        masks.append(((iC >> sh) == (jC >> sh)) & ((iC & (2 * s - 1)) >= s) & ((jC & (2 * s - 1)) < s))
    sel = jnp.concatenate(sel, axis=0)
    ng = ng_ref[...]

    for c in range(TB // C):
        lo = c * C
        qc, kc, vc, gc = q[lo:lo + C], k[lo:lo + C], v[lo:lo + C], gk[lo:lo + C]
        B = _dot(lower, gc, HI)
        R = _dot(sel, B, HI)
        A = jnp.where(eye, _dot_nt(qc, kc, HI), 0.0)
        for li in range(len(GLA_LEVELS)):
            r = R[li * C:(li + 1) * C]
            qs = qc * jnp.exp(jnp.minimum(B - r, 0.0))
            ks = kc * jnp.exp(jnp.minimum(r - B, 0.0))
            A = A + jnp.where(masks[li], _dot_nt(qs, ks, HI), 0.0)
        St = St_ref[...]
        o = _dot_nt(qc * jnp.exp(B), St, HI) + _dot(A, vc, HI)
        B_last = B[C - 1:C]
        St_ref[...] = St * jnp.exp(B_last) + _dot_tn(vc, kc * jnp.exp(B_last - B), HI)
        rows = min(C, rows_in)
        o = o[:rows]
        o = o * lax.rsqrt(jnp.mean(o * o, axis=-1, keepdims=True) + EPS) * ng
        o_ref[lo:lo + rows, :] = (o * _silu(r_ref[lo:lo + rows, :])).astype(o_ref.dtype)

    @pl.when(b == nb - 1)
    def _():
        sout_ref[0, 0] = St_ref[...]


def _gla(u, wg_pad, bg_pad, norm_g, s0t, layer, *, n_seq, rows_in, TB, t_valid):
    M = u.shape[0]
    nb = (M // n_seq) // rows_in
    H = C_HEADS
    qk_blk = C_DKP // C_DKP
    del qk_blk

    def qk_spec(off):
        return pl.BlockSpec((rows_in, C_DKP), lambda n, h, b: (n * nb + b, off // C_DKP + h))

    def v_spec(off):
        return pl.BlockSpec((rows_in, C_DV), lambda n, h, b: (n * nb + b, off // C_DV + h))

    state_spec = pl.BlockSpec((1, 1, C_DV, C_DKP), lambda n, h, b: (n, h, 0, 0))
    return pl.pallas_call(
        functools.partial(_gla_body, rows_in=rows_in, TB=TB, t_valid=t_valid),
        grid=(n_seq, H, nb),
        in_specs=[qk_spec(OFF_CQ), qk_spec(OFF_CK), v_spec(OFF_CV), v_spec(OFF_CR),
                  pl.BlockSpec((rows_in, LANES), lambda n, h, b: (n * nb + b, OFF_SMALL // LANES)),
                  pl.BlockSpec((1, LANES, C_DKP), lambda n, h, b: (layer * H + h, 0, 0)),
                  pl.BlockSpec((1, 1, C_DKP), lambda n, h, b: (layer * H + h, 0, 0)),
                  pl.BlockSpec((1, C_DV), lambda n, h, b: (0, 0)),
                  state_spec],
        out_specs=[pl.BlockSpec((rows_in, C_DV), lambda n, h, b: (n * nb + b, h)), state_spec],
        out_shape=[jax.ShapeDtypeStruct((M, C_V_W), BF16),
                   jax.ShapeDtypeStruct((n_seq, H, C_DV, C_DKP), F32)],
        scratch_shapes=[pltpu.VMEM((C_DV, C_DKP), F32)],
        compiler_params=_params(("parallel", "parallel", "arbitrary")),
        name="gla_mixer",
    )(u, u, u, u, u, wg_pad, bg_pad, norm_g.reshape(1, C_DV), s0t)


NEG = -1e30


def _rope_rows(x, cosf, sinf):
    return x * cosf + pltpu.roll(x, shift=B_HD // 2, axis=1) * sinf


def _b_off(g, r):
    return OFF_B // LANES + (g * 3 + r) * B_HEADS


def _dil_prompt_body(q0, k0, v0, q1, k1, v1, q2, k2, v2, cos_ref, sin_ref,
                     o_ref, ko0, vo0, ko1, vo1, ko2, vo2, qs, ks, os_, ls, *, T):
    qrefs, krefs, vrefs = (q0, q1, q2), (k0, k1, k2), (v0, v1, v2)
    korefs, vorefs = (ko0, ko1, ko2), (vo0, vo1, vo2)
    scale = B_HD ** -0.5
    RB = 256

    def rope_chunk(i, carry):
        rows = pl.ds(pl.multiple_of(i * RB, RB), RB)
        cosf, sinf = cos_ref[rows, :], sin_ref[rows, :]
        for g in range(3):
            qs[g, rows, :] = _rope_rows(qrefs[g][rows, :], cosf, sinf) * scale
            ks[g, rows, :] = _rope_rows(krefs[g][rows, :], cosf, sinf)
        return carry

    lax.fori_loop(0, T // RB, rope_chunk, 0)
    for g, (window, _) in enumerate(B_GROUPS):
        keep = min(window, T)
        korefs[g][0] = ks[g, T - keep:T, :]
        vorefs[g][0] = vrefs[g][T - keep:T, :]

    Q = B_NBACK

    def attend(g, q_rows, k_rows, nk, dist0):
        qb = qs[g, q_rows, :].astype(BF16)
        kb = ks[g, k_rows, :].astype(BF16)
        vb = vrefs[g][k_rows, :].astype(BF16)
        s = _dot_nt(qb, kb)
        dist = dist0 + lax.broadcasted_iota(jnp.int32, (Q, nk), 0) - lax.broadcasted_iota(jnp.int32, (Q, nk), 1)
        s = jnp.where((dist >= 0) & (dist <= B_NBACK), s, NEG)
        m = jnp.max(s, axis=-1, keepdims=True)
        p = jnp.exp(s - m)
        l = jnp.sum(p, axis=-1, keepdims=True)
        o = _dot(p.astype(BF16), vb) / l
        os_[g, q_rows, :] = o
        ls[g, q_rows, :] = jnp.broadcast_to(m + jnp.log(l), (Q, LANES))

    for g, (window, dil) in enumerate(B_GROUPS):
        n = T // dil
        nblk = n // Q
        if nblk == 1:
            def body(r, carry, g=g, dil=dil):
                rows = pl.ds(r, Q, stride=dil) if dil > 1 else pl.ds(0, Q)
                attend(g, rows, rows, Q, 0)
                return carry
            lax.fori_loop(0, dil, body, 0)
        else:
            def body(i, carry, g=g, dil=dil, nblk=nblk):
                r = i // nblk
                blk = i % nblk
                k0blk = jnp.maximum(blk - 1, 0)
                if dil > 1:
                    q_rows = pl.ds(blk * (Q * dil) + r, Q, stride=dil)
                    k_rows = pl.ds(k0blk * (Q * dil) + r, 2 * Q, stride=dil)
                else:
                    q_rows = pl.ds(pl.multiple_of(blk * Q, Q), Q)
                    k_rows = pl.ds(pl.multiple_of(k0blk * Q, Q), 2 * Q)
                attend(g, q_rows, k_rows, 2 * Q, (blk - k0blk) * Q)
                return carry
            lax.fori_loop(0, dil * nblk, body, 0)

    def combine(i, carry):
        rows = pl.ds(pl.multiple_of(i * RB, RB), RB)
        l0, l1, l2 = ls[0, rows, :], ls[1, rows, :], ls[2, rows, :]
        m = jnp.maximum(jnp.maximum(l0, l1), l2)
        e0, e1, e2 = jnp.exp(l0 - m), jnp.exp(l1 - m), jnp.exp(l2 - m)
        o = (e0 * os_[0, rows, :] + e1 * os_[1, rows, :] + e2 * os_[2, rows, :]) / (e0 + e1 + e2)
        o_ref[rows, :] = o.astype(o_ref.dtype)
        return carry

    lax.fori_loop(0, T // RB, combine, 0)


def _rope_tables(pos):
    half = B_HD // 2
    inv = ROPE_THETA ** (-jnp.arange(half, dtype=F32) / half)
    ang = pos.astype(F32)[:, None] * inv[None, :]
    cos, sin = jnp.cos(ang), jnp.sin(ang)
    return jnp.concatenate([cos, cos], axis=-1), jnp.concatenate([-sin, sin], axis=-1)


def _dil_prompt(u, cosf, sinf, *, n_seq, T):
    M = u.shape[0]
    H = B_HEADS

    def x_spec(g, r):
        return pl.BlockSpec((T, LANES), lambda n, h: (n, _b_off(g, r) + h))

    in_specs = [x_spec(g, r) for g in range(3) for r in range(3)]
    in_specs += [pl.BlockSpec((T, LANES), lambda n, h: (0, 0))] * 2
    out_specs = [pl.BlockSpec((T, LANES), lambda n, h: (n, h))]
    out_shape = [jax.ShapeDtypeStruct((M, B_W), BF16)]
    for window, _ in B_GROUPS:
        keep = min(window, T)
        for _kv in range(2):
            out_specs.append(pl.BlockSpec((1, keep, LANES), lambda n, h: (n, 0, h)))
            out_shape.append(jax.ShapeDtypeStruct((n_seq, keep, B_W), F32))
    return pl.pallas_call(
        functools.partial(_dil_prompt_body, T=T),
        grid=(n_seq, H),
        in_specs=in_specs,
        out_specs=out_specs,
        out_shape=out_shape,
        scratch_shapes=[pltpu.VMEM((3, T, LANES), F32)] * 4,
        compiler_params=_params(("parallel", "parallel")),
        name="dilated_attn_prompt",
    )(*([u] * 9), cosf, sinf)


def _dil_sample_body(q0, k0, v0, q1, k1, v1, q2, k2, v2, kc0, vc0, kc1, vc1, kc2, vc2, cos_ref, sin_ref,
                     o_ref, ko0, ko1, ko2, *, R):
    qrefs, krefs, vrefs = (q0, q1, q2), (k0, k1, k2), (v0, v1, v2)
    kcs, vcs, korefs = (kc0, kc1, kc2), (vc0, vc1, vc2), (ko0, ko1, ko2)
    cosf, sinf = cos_ref[...], sin_ref[...]
    scale = B_HD ** -0.5
    outs, lses = [], []
    for g, (window, dil) in enumerate(B_GROUPS):
        L = kcs[g].shape[1]
        q = _rope_rows(qrefs[g][...], cosf, sinf) * scale
        k = _rope_rows(krefs[g][...], cosf, sinf)
        korefs[g][...] = k
        kpad = jnp.concatenate([k, jnp.zeros((LANES - R, B_HD), F32)], axis=0)
        vpad = jnp.concatenate([vrefs[g][...], jnp.zeros((LANES - R, B_HD), F32)], axis=0)
        s1 = _dot_nt(q, kcs[g][0], HI)
        d1 = L + lax.broadcasted_iota(jnp.int32, (R, L), 0) - lax.broadcasted_iota(jnp.int32, (R, L), 1)
        s1 = jnp.where(((d1 & (dil - 1)) == 0) & (d1 <= dil * B_NBACK), s1, NEG)
        s2 = _dot_nt(q, kpad, HI)
        d2 = lax.broadcasted_iota(jnp.int32, (R, LANES), 0) - lax.broadcasted_iota(jnp.int32, (R, LANES), 1)
        s2 = jnp.where((d2 >= 0) & ((d2 & (dil - 1)) == 0) & (d2 <= dil * B_NBACK), s2, NEG)
        m = jnp.maximum(jnp.max(s1, axis=-1, keepdims=True), jnp.max(s2, axis=-1, keepdims=True))
        p1, p2 = jnp.exp(s1 - m), jnp.exp(s2 - m)
        l = jnp.sum(p1, axis=-1, keepdims=True) + jnp.sum(p2, axis=-1, keepdims=True)
        outs.append((_dot(p1, vcs[g][0], HI) + _dot(p2, vpad, HI)) / l)
        lses.append(m + jnp.log(l))
    m = jnp.maximum(jnp.maximum(lses[0], lses[1]), lses[2])
    e = [jnp.exp(x - m) for x in lses]
    o = (e[0] * outs[0] + e[1] * outs[1] + e[2] * outs[2]) / (e[0] + e[1] + e[2])
    o_ref[...] = o.astype(o_ref.dtype)


def _dil_sample(u, caches, cosf, sinf, layer, *, n_seq, R):
    M = u.shape[0]
    H = B_HEADS

    def x_spec(g, r):
        return pl.BlockSpec((R, LANES), lambda n, h: (n, _b_off(g, r) + h))

    in_specs = [x_spec(g, r) for g in range(3) for r in range(3)]
    args = [u] * 9
    for g in range(3):
        L = caches[g].shape[1]
        in_specs.append(pl.BlockSpec((1, L, LANES), lambda n, h: (layer * n_seq + n, 0, h)))
        in_specs.append(pl.BlockSpec((1, L, LANES), lambda n, h: (layer * n_seq + n, 0, H + h)))
        args += [caches[g], caches[g]]
    in_specs += [pl.BlockSpec((R, LANES), lambda n, h: (0, 0))] * 2
    args += [cosf, sinf]
    row_spec = pl.BlockSpec((R, LANES), lambda n, h: (n, h))
    return pl.pallas_call(
        functools.partial(_dil_sample_body, R=R),
        grid=(n_seq, H),
        in_specs=in_specs,
        out_specs=[row_spec] * 4,
        out_shape=[jax.ShapeDtypeStruct((M, B_W), BF16)] + [jax.ShapeDtypeStruct((M, B_W), F32)] * 3,
        compiler_params=_params(("parallel", "parallel")),
        name="dilated_attn_sample",
    )(*args)


def _prepare_weights(w_ada, w_in, c_w_gate2, c_b_gate, w_br_a, w_br_b, w_br_c, w_o, f_up, f_down):
    o = 0
    parts = {}
    for name, width in (("a_qkv", A_QKV_W), ("a_z", A_V_W), ("a_b", A_HEADS), ("a_a", A_HEADS), ("b_qkv", B_QKV_W),
                        ("c_q", C_QK_W), ("c_k", C_QK_W), ("c_v", C_V_W), ("c_r", C_V_W), ("c_glr", C_RANK),
                        ("gates", N_BRANCH * D_MODEL)):
        parts[name] = w_in[:, :, o:o + width].astype(BF16)
        o += width

    def pad_heads(w):
        w = w.reshape(DEPTH, D_MODEL, C_HEADS, C_DK)
        return jnp.pad(w, ((0, 0), (0, 0), (0, 0), (0, C_DKP - C_DK))).reshape(DEPTH, D_MODEL, C_HEADS * C_DKP)

    tail = N_INP - OFF_SMALL - 2 * A_HEADS - C_RANK
    w_in_p = jnp.concatenate(
        [parts["a_qkv"], parts["c_v"], parts["c_r"], parts["a_z"], parts["b_qkv"], pad_heads(parts["c_q"]),
         pad_heads(parts["c_k"]), parts["gates"], parts["a_b"], parts["a_a"], parts["c_glr"],
         jnp.zeros((DEPTH, D_MODEL, tail), BF16)], axis=-1)

    wg = c_w_gate2.reshape(DEPTH, C_RANK, C_HEADS, C_DK).transpose(0, 2, 1, 3)
    wg = jnp.pad(wg, ((0, 0), (0, 0), (SM_GLR, LANES - SM_GLR - C_RANK), (0, C_DKP - C_DK)))
    wg = wg.reshape(DEPTH * C_HEADS, LANES, C_DKP)
    bg = jnp.pad(c_b_gate.reshape(DEPTH, C_HEADS, 1, C_DK), ((0, 0), (0, 0), (0, 0), (0, C_DKP - C_DK)))
    bg = bg.reshape(DEPTH * C_HEADS, 1, C_DKP)
    cast = lambda w: w.astype(BF16)
    return dict(w_ada=cast(w_ada), w_in=w_in_p, wg=wg, bg=bg, w_br_a=cast(w_br_a), w_br_b=cast(w_br_b),
                w_br_c=cast(w_br_c), w_o=cast(w_o), f_up=cast(f_up), f_down=cast(f_down))


def _trunk(x, mods, n_seq, T, t_valid, states, W, P, cosf, sinf):
    M = n_seq * T
    prompt = states is None
    TM = 1024
    outs = {k: [] for k in ("aS", "aconv", "b0", "b1", "b2", "cS", "fconv")}
    tv = T if t_valid is None else t_valid
    for l in range(DEPTH):
        mod = mods[l]
        if prompt:
            sh1, sc1, g1, sh2, sc2, g2 = (mod[:, i].reshape(n_seq, 1, D_MODEL) for i in range(6))
            aS0 = jnp.zeros((n_seq, A_HEADS, A_DK, A_DV), F32)
            abuf0 = jnp.zeros((n_seq, A_CONV - 1, A_QKV_W), F32)
            cS0t = jnp.zeros((n_seq, C_HEADS, C_DV, C_DKP), F32)
            fbuf0 = jnp.zeros((n_seq, F_CONV - 1, 2 * D_FF), F32)
        else:
            sh1, sc1, g1, sh2, sc2, g2 = (jnp.repeat(mod[:, i], T, axis=0) for i in range(6))
            st_aS, st_aconv, st_b, st_cS, st_fconv = states
            aS0, abuf0, fbuf0 = st_aS[l], st_aconv[l], st_fconv[l]
            cS0t = jnp.pad(jnp.swapaxes(st_cS[l], -1, -2), ((0, 0), (0, 0), (0, 0), (0, C_DKP - C_DK)))

        h = _norm(x, W["norm1_g"][l], sc1, sh1, seq_len=T, out_dtype=BF16)
        u = _matmul(h, P["w_in"], l, tm=TM, tn=TN_IN, name="in_proj")
        u3 = u.reshape(n_seq, T, N_INP)

        small = u3[:, :, OFF_SMALL:OFF_SMALL + LANES]
        tp = CHUNK if T < CHUNK else T
        ab_t = jnp.swapaxes(small[:, :, SM_AB:SM_AB + A_HEADS], 1, 2)
        aa_t = jnp.swapaxes(small[:, :, SM_AA:SM_AA + A_HEADS], 1, 2)
        if tp > T:
            ab_t = jnp.pad(ab_t, ((0, 0), (0, 0), (0, tp - T)))
            aa_t = jnp.pad(aa_t, ((0, 0), (0, 0), (0, tp - T)))
        ab_t = ab_t.reshape(n_seq, A_HEADS, 1, tp)
        aa_t = aa_t.reshape(n_seq, A_HEADS, 1, tp)
        rows_in = min(T, 256)
        TB = max(rows_in, CHUNK)
        oa, aS = _gdn(u, ab_t, aa_t, W["a_conv_w"], abuf0, W["a_A_log"][l], W["a_dt_bias"][l], W["a_norm_g"][l],
                      aS0, l, n_seq=n_seq, rows_in=rows_in, TB=TB, t_valid=t_valid)
        outs["aS"].append(aS)
        outs["aconv"].append(u3[:, tv - (A_CONV - 1):tv, OFF_AQKV:OFF_AQKV + A_QKV_W])

        if prompt:
            res = _dil_prompt(u, cosf, sinf, n_seq=n_seq, T=T)
            ob = res[0]
            for g in range(3):
                kk, vv = res[1 + 2 * g], res[2 + 2 * g]
                keep = kk.shape[1]
                outs["b%d" % g].append(jnp.stack([kk, vv], axis=2).reshape(n_seq, keep, 2, B_HEADS, B_HD))
        else:
            ob, kn0, kn1, kn2 = _dil_sample(u, st_b, cosf, sinf, l, n_seq=n_seq, R=T)
            for g, kn in enumerate((kn0, kn1, kn2)):
                off = OFF_B + (g * 3 + 2) * B_W
                kk = kn.reshape(n_seq, T, B_W)[:, :tv]
                vv = u3[:, :tv, off:off + B_W]
                outs["b%d" % g].append(jnp.stack([kk, vv], axis=2).reshape(n_seq, tv, 2, B_HEADS, B_HD))

        oc, cSt = _gla(u, P["wg"], P["bg"], W["c_norm_g"][l], cS0t, l, n_seq=n_seq, rows_in=rows_in, TB=TB,
                       t_valid=t_valid)
        outs["cS"].append(jnp.swapaxes(cSt, -1, -2)[:, :, :C_DK, :])

        merged = _merge(oa, ob, oc, P["w_br_a"], P["w_br_b"], P["w_br_c"], u, l, tm=512, tn=512)
        x = _matmul(merged, P["w_o"], l, tm=TM, tn=512, seq_len=T, resid=x, gate=g1, name="out_proj")

        h2 = _norm(x, W["norm2_g"][l], sc2, sh2, seq_len=T, out_dtype=BF16)
        up = _matmul(h2, P["f_up"], l, tm=TM, tn=512, name="ffn_up")
        outs["fconv"].append(up.reshape(n_seq, T, 2 * D_FF)[:, tv - (F_CONV - 1):tv])
        act = _ffn_act(up, W["f_conv_w"], W["f_conv_b"], fbuf0, l, n_seq=n_seq, seq_len=T)
        x = _matmul(act, P["f_down"], l, tm=512, tn=256, seq_len=T, resid=x, gate=g2, name="ffn_down")

    y = _norm(x, W["final_g"], None, None, seq_len=T, out_dtype=F32)
    return y, tuple(jnp.stack(outs[k]) for k in ("aS", "aconv", "b0", "b1", "b2", "cS", "fconv"))


def kernel(x_prompt, x_sample, state_a_S, state_a_conv, cache_b1_kv, cache_b2_kv, cache_b3_kv, state_c_S,
           state_ffn_conv, c_prompt, c_sample, norm1_g, norm2_g, w_ada, b_ada, w_in, a_conv_w, a_A_log, a_dt_bias,
           a_norm_g, c_w_gate2, c_b_gate, c_norm_g, w_br_a, w_br_b, w_br_c, w_o, f_up, f_conv_w, f_conv_b, f_down,
           final_g):
    n_p, T, _ = x_prompt.shape
    n_s, t_s, _ = x_sample.shape
    R = SUBLANES
    P = _prepare_weights(w_ada, w_in, c_w_gate2, c_b_gate, w_br_a, w_br_b, w_br_c, w_o, f_up, f_down)
    W = dict(norm1_g=norm1_g, norm2_g=norm2_g, a_conv_w=a_conv_w, a_A_log=a_A_log, a_dt_bias=a_dt_bias,
             a_norm_g=a_norm_g, c_norm_g=c_norm_g, f_conv_w=f_conv_w,
             f_conv_b=f_conv_b.reshape(DEPTH, 1, 2 * D_FF), final_g=final_g)

    n_c = n_p + n_s
    c_all = jnp.pad(jnp.concatenate([c_prompt, c_sample], axis=0), ((0, 2 * SUBLANES - n_c), (0, 0)))
    b_ada3 = b_ada.reshape(DEPTH, 1, 6 * D_MODEL)
    mods_p, mods_s = [], []
    for l in range(DEPTH):
        mod = _matmul(c_all, P["w_ada"], l, tm=2 * SUBLANES, tn=1024, a_silu=True, bias=b_ada3, name="ada_mod")
        mods_p.append(mod[:n_p].reshape(n_p, 6, D_MODEL))
        mods_s.append(mod[n_p:n_c].reshape(n_s, 6, D_MODEL))

    cos_p, sin_p = _rope_tables(jnp.arange(T, dtype=jnp.int32))
    cos_s, sin_s = _rope_tables(PAST_LEN + jnp.arange(R, dtype=jnp.int32))

    y_p, st_p = _trunk(x_prompt.reshape(n_p * T, D_MODEL), mods_p, n_p, T, None, None, W, P, cos_p, sin_p)

    xs = jnp.pad(x_sample, ((0, 0), (0, R - t_s), (0, 0))).reshape(n_s * R, D_MODEL)
    caches = tuple(c.reshape(DEPTH * n_s, c.shape[2], 2 * B_W) for c in (cache_b1_kv, cache_b2_kv, cache_b3_kv))
    states = (state_a_S, state_a_conv, caches, state_c_S, state_ffn_conv)
    y_s, st_s = _trunk(xs, mods_s, n_s, R, t_s, states, W, P, cos_s, sin_s)
    y_s = y_s.reshape(n_s, R, D_MODEL)[:, :t_s]

    return (y_p.reshape(n_p, T, D_MODEL), y_s) + st_p + st_s
```
